```python
import math
import jax, jax.numpy as jnp
from jax import lax
import numpy as np

D_MODEL = 1024
BATCH = 8
SEQ = 2048
DEPTH = 4

HEAD_DIM = 64
BLOCK = 128
EPS = 1e-6
ROPE_BASE = 10000.0
A_HEADS = 6
A_KV_HEADS = 2
WINDOW = 128
B_HEADS = 6
B_KV_HEADS = 2
GRID_W = 64
C_HEADS = 4
C_Q_RANK = 256
C_KV_RANK = 128
C_NOPE = 64
C_ROPE = 32
C_V = 64
D_FF = 4 * D_MODEL

A_Q = A_HEADS * HEAD_DIM
A_KV = A_KV_HEADS * HEAD_DIM
B_Q = B_HEADS * HEAD_DIM
B_KV = B_KV_HEADS * HEAD_DIM
A_COLS = A_Q + 2 * A_KV
B_COLS = B_Q + 2 * B_KV
C_COLS = C_Q_RANK + C_KV_RANK + C_ROPE
IN_COLS = A_COLS + B_COLS + C_COLS
MIX_WIDTH = A_Q + B_Q + C_HEADS * C_V

kernel_name = "hybrid_parallel_heads_encoder"


def rmsnorm(x, g):
    xf = x.astype(jnp.float32)
    y = xf * lax.rsqrt(jnp.mean(xf * xf, axis=-1, keepdims=True) + EPS)
    return (y * g.astype(jnp.float32)).astype(x.dtype)


def rope_tables(pos, dim):
    inv = ROPE_BASE ** (-jnp.arange(0, dim, 2, dtype=jnp.float32) / dim)
    ang = pos.astype(jnp.float32)[:, None] * inv[None, :]
    ang = jnp.concatenate([ang, ang], axis=-1)
    return jnp.cos(ang), jnp.sin(ang)


def apply_rope(x, cos, sin):
    half = x.shape[-1] // 2
    x1, x2 = x[..., :half], x[..., half:]
    rot = jnp.concatenate([-x2, x1], axis=-1)
    return (x.astype(jnp.float32) * cos + rot.astype(jnp.float32) * sin).astype(x.dtype)


def alibi_slopes(n):
    return 2.0 ** (-8.0 * jnp.arange(1, n + 1, dtype=jnp.float32) / n)


def windowed_gqa_sink(q, k, v, sink):
    b, s, hq, d = q.shape
    hkv = k.shape[2]
    g = hq // hkv
    nb = s // BLOCK
    qb = q.reshape(b, nb, BLOCK, hkv, g, d)
    pad = ((0, 0), (BLOCK, BLOCK), (0, 0), (0, 0))
    kp = jnp.pad(k, pad).reshape(b, nb + 2, BLOCK, hkv, d)
    vp = jnp.pad(v, pad).reshape(b, nb + 2, BLOCK, hkv, d)
    kb = jnp.concatenate([kp[:, :-2], kp[:, 1:-1], kp[:, 2:]], axis=2)
    vb = jnp.concatenate([vp[:, :-2], vp[:, 1:-1], vp[:, 2:]], axis=2)
    sc = jnp.einsum('bnqhgd,bnkhd->bnhgqk', qb, kb).astype(jnp.float32) * (d ** -0.5)
    blk = jnp.arange(nb)[:, None] * BLOCK
    qpos = blk + jnp.arange(BLOCK)[None, :]
    kpos = blk - BLOCK + jnp.arange(3 * BLOCK)[None, :]
    dist = jnp.abs(qpos[:, :, None] - kpos[:, None, :])
    valid = (dist <= WINDOW) & ((kpos >= 0) & (kpos < s))[:, None, :]
    slopes = alibi_slopes(hq).reshape(hkv, g)
    bias = -slopes[None, :, :, None, None] * dist[:, None, None].astype(jnp.float32)
    sc = jnp.where(valid[:, None, None], sc + bias, -1e30)
    sk = sink.astype(jnp.float32).reshape(hkv, g)[:, :, None]
    m = jnp.maximum(jnp.max(sc, axis=-1), sk)
    p = jnp.exp(sc - m[..., None])
    probs = p / (jnp.sum(p, axis=-1) + jnp.exp(sk - m))[..., None]
    out = jnp.einsum('bnhgqk,bnkhd->bnqhgd', probs.astype(v.dtype), vb)
    return out.reshape(b, s, hq * d)


def dense_gqa_blocks(q, k, v):
    b, s, hq, d = q.shape
    hkv = k.shape[2]
    g = hq // hkv
    nb = s // BLOCK
    qb = q.reshape(b, nb, BLOCK, hkv, g, d).transpose(1, 0, 2, 3, 4, 5)

    def one(qblk):
        sc = jnp.einsum('bqhgd,bkhd->bhgqk', qblk, k).astype(jnp.float32) * (d ** -0.5)
        p = jax.nn.softmax(sc, axis=-1).astype(v.dtype)
        return jnp.einsum('bhgqk,bkhd->bqhgd', p, v)

    out = lax.map(one, qb)
    return out.transpose(1, 0, 2, 3, 4, 5).reshape(b, s, hq * d)


def mla_blocks(q_nope, q_rope, k_nope, k_rope, v):
    b, s, h, dn = q_nope.shape
    dr = q_rope.shape[-1]
    nb = s // BLOCK
    scale = (dn + dr) ** -0.5
    qn = q_nope.reshape(b, nb, BLOCK, h, dn).transpose(1, 0, 2, 3, 4)
    qr = q_rope.reshape(b, nb, BLOCK, h, dr).transpose(1, 0, 2, 3, 4)

    def one(args):
        qn_b, qr_b = args
        sc = (jnp.einsum('bqhd,bkhd->bhqk', qn_b, k_nope)
              + jnp.einsum('bqhd,bkd->bhqk', qr_b, k_rope)).astype(jnp.float32) * scale
        p = jax.nn.softmax(sc, axis=-1).astype(v.dtype)
        return jnp.einsum('bhqk,bkhd->bqhd', p, v)

    out = lax.map(one, (qn, qr))
    return out.transpose(1, 0, 2, 3, 4).reshape(b, s, h * v.shape[-1])


def setup_inputs(seed: int = 0) -> dict:
    key = jax.random.key(seed)
    ks = jax.random.split(key, 16)
    f32 = jnp.float32

    def nrm(k, shape, fan_in):
        return jax.random.normal(k, shape, f32) * (fan_in ** -0.5)

    def gain(k, shape):
        return 1.0 + 0.05 * jax.random.normal(k, shape, f32)

    return {
        "x": jax.random.normal(ks[0], (BATCH, SEQ, D_MODEL), f32),
        "attn_norm": gain(ks[1], (DEPTH, D_MODEL)),
        "w_in": nrm(ks[2], (DEPTH, D_MODEL, IN_COLS), D_MODEL),
        "a_sink": 0.5 * jax.random.normal(ks[3], (DEPTH, A_HEADS), f32),
        "b_q_norm": gain(ks[4], (DEPTH, HEAD_DIM)),
        "b_k_norm": gain(ks[5], (DEPTH, HEAD_DIM)),
        "c_q_norm": gain(ks[6], (DEPTH, C_Q_RANK)),
        "c_kv_norm": gain(ks[7], (DEPTH, C_KV_RANK)),
        "w_uq": nrm(ks[8], (DEPTH, C_Q_RANK, C_HEADS * (C_NOPE + C_ROPE)), C_Q_RANK),
        "w_ukv": nrm(ks[9], (DEPTH, C_KV_RANK, C_HEADS * (C_NOPE + C_V)), C_KV_RANK),
        "w_out": nrm(ks[10], (DEPTH, MIX_WIDTH, D_MODEL), MIX_WIDTH),
        "mlp_norm": gain(ks[11], (DEPTH, D_MODEL)),
        "w_ff1": nrm(ks[12], (DEPTH, D_MODEL, D_FF), D_MODEL),
        "w_ff2": nrm(ks[13], (DEPTH, D_FF, D_MODEL), D_FF),
        "final_norm": gain(ks[14], (D_MODEL,)),
    }


def reference(x, attn_norm, w_in, a_sink, b_q_norm, b_k_norm, c_q_norm, c_kv_norm,
              w_uq, w_ukv, w_out, mlp_norm, w_ff1, w_ff2, final_norm):
    b, s, _ = x.shape
    rows = s // GRID_W
    pos = jnp.arange(s)
    row_pos = jnp.repeat(jnp.arange(rows), GRID_W)
    col_pos = jnp.tile(jnp.arange(GRID_W), rows)
    half = HEAD_DIM // 2
    cos_r, sin_r = rope_tables(row_pos, half)
    cos_c, sin_c = rope_tables(col_pos, half)
    cos_m, sin_m = rope_tables(pos, C_ROPE)

    def axial(t):
        tr = apply_rope(t[..., :half], cos_r[:, None, :], sin_r[:, None, :])
        tc = apply_rope(t[..., half:], cos_c[:, None, :], sin_c[:, None, :])
        return jnp.concatenate([tr, tc], axis=-1)

    o1 = A_Q
    o2 = o1 + A_KV
    o3 = o2 + A_KV
    o4 = o3 + B_Q
    o5 = o4 + B_KV
    o6 = o5 + B_KV
    o7 = o6 + C_Q_RANK
    o8 = o7 + C_KV_RANK

    for l in range(DEPTH):
        h = rmsnorm(x, attn_norm[l])
        p = h @ w_in[l]

        qa = p[..., :o1].reshape(b, s, A_HEADS, HEAD_DIM)
        ka = p[..., o1:o2].reshape(b, s, A_KV_HEADS, HEAD_DIM)
        va = p[..., o2:o3].reshape(b, s, A_KV_HEADS, HEAD_DIM)
        out_a = windowed_gqa_sink(qa, ka, va, a_sink[l])

        qb = rmsnorm(p[..., o3:o4].reshape(b, s, B_HEADS, HEAD_DIM), b_q_norm[l])
        kb = rmsnorm(p[..., o4:o5].reshape(b, s, B_KV_HEADS, HEAD_DIM), b_k_norm[l])
        vb = p[..., o5:o6].reshape(b, s, B_KV_HEADS, HEAD_DIM)
        out_b = dense_gqa_blocks(axial(qb), axial(kb), vb)

        cq = rmsnorm(p[..., o6:o7], c_q_norm[l]) @ w_uq[l]
        cq = cq.reshape(b, s, C_HEADS, C_NOPE + C_ROPE)
        q_nope = cq[..., :C_NOPE]
        q_rope = apply_rope(cq[..., C_NOPE:], cos_m[:, None, :], sin_m[:, None, :])
        ckv = rmsnorm(p[..., o7:o8], c_kv_norm[l]) @ w_ukv[l]
        ckv = ckv.reshape(b, s, C_HEADS, C_NOPE + C_V)
        k_nope = ckv[..., :C_NOPE]
        vc = ckv[..., C_NOPE:]
        k_rope = apply_rope(p[..., o8:], cos_m, sin_m)
        out_c = mla_blocks(q_nope, q_rope, k_nope, k_rope, vc)

        mixed = jnp.concatenate([out_a, out_b, out_c], axis=-1)
        x = x + mixed @ w_out[l]

        h2 = rmsnorm(x, mlp_norm[l])
        x = x + jnp.square(jax.nn.relu(h2 @ w_ff1[l])) @ w_ff2[l]

    return rmsnorm(x, final_norm)
```

```python
import functools
import math

import numpy as np
import jax
import jax.numpy as jnp
from jax import lax
from jax.experimental import pallas as pl
from jax.experimental.pallas import tpu as pltpu

D_MODEL = 1024
HEAD_DIM = 64
BLOCK = 128
EPS = 1e-6
ROPE_BASE = 10000.0
A_HEADS, A_KV_HEADS, WINDOW = 6, 2, 128
B_HEADS, B_KV_HEADS, GRID_W = 6, 2, 64
C_HEADS, C_Q_RANK, C_KV_RANK, C_NOPE, C_ROPE, C_V = 4, 256, 128, 64, 32, 64
D_FF = 4 * D_MODEL

A_Q = A_HEADS * HEAD_DIM
A_KV = A_KV_HEADS * HEAD_DIM
B_Q = B_HEADS * HEAD_DIM
B_KV = B_KV_HEADS * HEAD_DIM
IN_COLS = 2 * (A_Q + 2 * A_KV) + C_Q_RANK + C_KV_RANK + C_ROPE

LANES = 128
IN_COLS_PAD = 1792
C_HEAD_PAD = 128
LOG2E = math.log2(math.e)
VMEM_LIMIT = 56 * 1024 * 1024

O1 = A_Q
O2 = O1 + A_KV
O3 = O2 + A_KV
O4 = O3 + B_Q
O5 = O4 + B_KV
O6 = O5 + B_KV
O7 = O6 + C_Q_RANK
O8 = O7 + C_KV_RANK

BF16 = jnp.bfloat16
F32 = jnp.float32


def _rmsnorm(x, g):
    return x * lax.rsqrt(jnp.mean(x * x, axis=-1, keepdims=True) + EPS) * g


def _rope(t, cos, sin_next, sin_prev):
    return (t * cos + pltpu.roll(t, LANES - 16, axis=1) * sin_next
            + pltpu.roll(t, 16, axis=1) * sin_prev)


def _group_sumsq(x, seg):
    sq = x * x
    hi = sq.astype(BF16)
    lo = (sq - hi.astype(F32)).astype(BF16)
    return (jnp.dot(hi, seg, preferred_element_type=F32)
            + jnp.dot(lo, seg, preferred_element_type=F32))


def _proj_in_kernel(x_ref, g_ref, w_ref, segq_ref, segk_ref, gq_ref, gk_ref,
                    cqn_ref, ckvn_ref, wuq_ref, wukvk_ref, wukvv_ref,
                    cosb_ref, snb_ref, spb_ref, cosc_ref, snc_ref, spc_ref,
                    qa_ref, ka_ref, va_ref, qb_ref, kb_ref, vb_ref,
                    qc_ref, kc_ref, vc_ref):
    x = x_ref[...]
    h = _rmsnorm(x, g_ref[...]).astype(BF16)
    p = jnp.dot(h, w_ref[...], preferred_element_type=F32)

    qa_ref[...] = (p[:, :O1] * (HEAD_DIM ** -0.5 * LOG2E)).astype(BF16)
    ka_ref[...] = p[:, O1:O2].astype(BF16)
    va_ref[...] = p[:, O2:O3].astype(BF16)

    cosb, snb, spb = cosb_ref[...], snb_ref[...], spb_ref[...]
    xq = p[:, O3:O4]
    qn = xq * lax.rsqrt(_group_sumsq(xq, segq_ref[...]) * (1.0 / HEAD_DIM) + EPS) * gq_ref[...]
    for c in range(B_Q // LANES):
        t = _rope(qn[:, c * LANES:(c + 1) * LANES], cosb, snb, spb)
        qb_ref[:, c * LANES:(c + 1) * LANES] = (t * (HEAD_DIM ** -0.5 * LOG2E)).astype(BF16)
    xk = p[:, O4:O5]
    kn = xk * lax.rsqrt(_group_sumsq(xk, segk_ref[...]) * (1.0 / HEAD_DIM) + EPS) * gk_ref[...]
    kb_ref[...] = _rope(kn, cosb, snb, spb).astype(BF16)
    vb_ref[...] = p[:, O5:O6].astype(BF16)

    cosc, snc, spc = cosc_ref[...], snc_ref[...], spc_ref[...]
    cq = jnp.dot(_rmsnorm(p[:, O6:O7], cqn_ref[...]).astype(BF16), wuq_ref[...],
                 preferred_element_type=F32)
    kvl = _rmsnorm(p[:, O7:O8], ckvn_ref[...]).astype(BF16)
    kn_c = jnp.dot(kvl, wukvk_ref[...], preferred_element_type=F32)
    vc_ref[...] = jnp.dot(kvl, wukvv_ref[...], preferred_element_type=F32).astype(BF16)
    kr = _rope(pltpu.roll(p[:, O8:O8 + LANES], C_NOPE, axis=1), cosc, snc, spc)
    c_scale = (C_NOPE + C_ROPE) ** -0.5 * LOG2E
    for hd in range(C_HEADS):
        sl = slice(hd * C_HEAD_PAD, (hd + 1) * C_HEAD_PAD)
        qc_ref[:, sl] = (_rope(cq[:, sl], cosc, snc, spc) * c_scale).astype(BF16)
        kc_ref[:, sl] = (kn_c[:, sl] + kr).astype(BF16)


def _attn_a_kernel(sink_ref, q_ref, k_ref, v_ref, o_ref, *, seq, slopes):
    n = pl.program_id(1)
    span = 3 * BLOCK
    start = pl.multiple_of(jnp.clip((n - 1) * BLOCK, 0, seq - span), BLOCK)
    rel = (lax.broadcasted_iota(jnp.int32, (BLOCK, span), 0)
           - lax.broadcasted_iota(jnp.int32, (BLOCK, span), 1))
    dist = jnp.abs(rel + (n * BLOCK - start))
    valid = dist <= WINDOW
    distf = dist.astype(F32)
    g = A_HEADS // A_KV_HEADS
    for kv in range(A_KV_HEADS):
        k = k_ref[pl.ds(start, span), kv * HEAD_DIM:(kv + 1) * HEAD_DIM]
        v = v_ref[pl.ds(start, span), kv * HEAD_DIM:(kv + 1) * HEAD_DIM]
        for gi in range(g):
            hd = kv * g + gi
            q = q_ref[:, hd * HEAD_DIM:(hd + 1) * HEAD_DIM]
            s = lax.dot_general(q, k, (((1,), (1,)), ((), ())), preferred_element_type=F32)
            s = jnp.where(valid, s - (slopes[hd] * LOG2E) * distf, -1e30)
            sink = sink_ref[hd] * LOG2E
            m = jnp.maximum(jnp.max(s, axis=-1, keepdims=True), sink)
            e = jnp.exp2(s - m)
            den = jnp.sum(e, axis=-1, keepdims=True) + jnp.exp2(sink - m)
            o = jnp.dot(e.astype(BF16), v, preferred_element_type=F32) / den
            o_ref[:, hd * HEAD_DIM:(hd + 1) * HEAD_DIM] = o.astype(BF16)


def _softmax_pv(s, v):
    m = jnp.max(s, axis=-1, keepdims=True)
    e = jnp.exp2(s - m)
    den = jnp.sum(e, axis=-1, keepdims=True)
    return jnp.dot(e.astype(BF16), v, preferred_element_type=F32) / den


def _attn_b_kernel(q_ref, k_ref, v_ref, o_ref, *, tq):
    g = B_HEADS // B_KV_HEADS
    for kv in range(B_KV_HEADS):
        k = k_ref[:, kv * HEAD_DIM:(kv + 1) * HEAD_DIM]
        v = v_ref[:, kv * HEAD_DIM:(kv + 1) * HEAD_DIM]
        q = jnp.concatenate(
            [q_ref[:, (kv * g + gi) * HEAD_DIM:(kv * g + gi + 1) * HEAD_DIM] for gi in range(g)],
            axis=0)
        s = lax.dot_general(q, k, (((1,), (1,)), ((), ())), preferred_element_type=F32)
        o = _softmax_pv(s, v)
        for gi in range(g):
            hd = kv * g + gi
            o_ref[:, hd * HEAD_DIM:(hd + 1) * HEAD_DIM] = o[gi * tq:(gi + 1) * tq].astype(BF16)


def _attn_c_kernel(q_ref, k_ref, v_ref, o_ref):
    for hd in range(C_HEADS):
        q = q_ref[:, hd * C_HEAD_PAD:(hd + 1) * C_HEAD_PAD]
        k = k_ref[:, hd * C_HEAD_PAD:(hd + 1) * C_HEAD_PAD]
        v = v_ref[:, hd * C_V:(hd + 1) * C_V]
        s = lax.dot_general(q, k, (((1,), (1,)), ((), ())), preferred_element_type=F32)
        o_ref[:, hd * C_V:(hd + 1) * C_V] = _softmax_pv(s, v).astype(BF16)


def _out_ffn_kernel(x_ref, oa_ref, ob_ref, oc_ref, wo_ref, g_ref, w1_ref, w2_ref, fg_ref,
                    y_ref, *, tf, final):
    mixed = jnp.concatenate([oa_ref[...], ob_ref[...], oc_ref[...]], axis=-1)
    x = x_ref[...] + jnp.dot(mixed, wo_ref[...], preferred_element_type=F32)
    h = _rmsnorm(x, g_ref[...]).astype(BF16)
    mlp = None
    for f in range(D_FF // tf):
        u = jnp.maximum(jnp.dot(h, w1_ref[:, f * tf:(f + 1) * tf], preferred_element_type=F32), 0.0)
        part = jnp.dot((u * u).astype(BF16), w2_ref[f * tf:(f + 1) * tf, :],
                       preferred_element_type=F32)
        mlp = part if mlp is None else mlp + part
    acc = x + mlp
    if final:
        acc = _rmsnorm(acc, fg_ref[...])
    y_ref[...] = acc


def _rope_np(pos, dim):
    inv = ROPE_BASE ** (-np.arange(0, dim, 2, dtype=np.float64) / dim)
    ang = pos.astype(np.float64)[:, None] * inv[None, :]
    ang = np.concatenate([ang, ang], axis=-1)
    return np.cos(ang), np.sin(ang)


def _split_sin(sin, width):
    first = (np.arange(width) % 32) < 16
    return np.where(first, -sin, 0.0), np.where(first, 0.0, sin)


def _rope_tables(seq):
    pos = np.arange(seq)
    half = HEAD_DIM // 2
    cr, sr = _rope_np(pos // GRID_W, half)
    cc, sc = _rope_np(pos % GRID_W, half)
    cos_b = np.tile(np.concatenate([cr, cc], axis=-1), (1, LANES // HEAD_DIM))
    sin_b = np.tile(np.concatenate([sr, sc], axis=-1), (1, LANES // HEAD_DIM))
    cm, sm = _rope_np(pos, C_ROPE)
    cos_c = np.ones((seq, C_HEAD_PAD))
    sin_c = np.zeros((seq, C_HEAD_PAD))
    cos_c[:, C_NOPE:C_NOPE + C_ROPE] = cm
    sin_c[:, C_NOPE:C_NOPE + C_ROPE] = sm
    out = []
    for cos, sin in ((cos_b, sin_b), (cos_c, sin_c)):
        sn, sp = _split_sin(sin, LANES)
        out += [jnp.asarray(cos, F32), jnp.asarray(sn, F32), jnp.asarray(sp, F32)]
    return out


def _const(shape):
    return pl.BlockSpec(shape, lambda *_: (0,) * len(shape), pipeline_mode=pl.Buffered(1))


def _layer_const(shape, l):
    return pl.BlockSpec((None,) + shape, lambda *_: (l,) + (0,) * len(shape),
                        pipeline_mode=pl.Buffered(1))


def _params(sem):
    return pltpu.CompilerParams(dimension_semantics=sem, vmem_limit_bytes=VMEM_LIMIT)


def kernel(x, attn_norm, w_in, a_sink, b_q_norm, b_k_norm, c_q_norm, c_kv_norm,
           w_uq, w_ukv, w_out, mlp_norm, w_ff1, w_ff2, final_norm):
    b, s, d = x.shape
    depth = w_in.shape[0]
    t = b * s
    tm = 512
    tq_b = 128
    tq_c = 256
    tf = 1024
    assert s % tm == 0 and s % tq_b == 0 and s % tq_c == 0 and s >= 3 * BLOCK

    w_in_b = jnp.pad(w_in.astype(BF16), ((0, 0), (0, 0), (0, IN_COLS_PAD - IN_COLS)))
    w_out_b = w_out.astype(BF16)
    w1_b = w_ff1.astype(BF16)
    w2_b = w_ff2.astype(BF16)
    uq = w_uq.astype(BF16).reshape(depth, C_Q_RANK, C_HEADS, C_NOPE + C_ROPE)
    uq = jnp.pad(uq, ((0, 0), (0, 0), (0, 0), (0, C_HEAD_PAD - C_NOPE - C_ROPE)))
    uq = uq.reshape(depth, C_Q_RANK, C_HEADS * C_HEAD_PAD)
    ukv = w_ukv.astype(BF16).reshape(depth, C_KV_RANK, C_HEADS, C_NOPE + C_V)
    ukv_k = jnp.pad(ukv[..., :C_NOPE], ((0, 0), (0, 0), (0, 0), (0, C_HEAD_PAD - C_NOPE)))
    ukv_k = ukv_k.reshape(depth, C_KV_RANK, C_HEADS * C_HEAD_PAD)
    ukv_v = ukv[..., C_NOPE:].reshape(depth, C_KV_RANK, C_HEADS * C_V)
    gq = jnp.tile(b_q_norm, (1, B_HEADS)).reshape(depth, 1, B_Q)
    gk = jnp.tile(b_k_norm, (1, B_KV_HEADS)).reshape(depth, 1, B_KV)
    attn_g = attn_norm.reshape(depth, 1, d)
    mlp_g = mlp_norm.reshape(depth, 1, d)
    cqn = c_q_norm.reshape(depth, 1, C_Q_RANK)
    ckvn = c_kv_norm.reshape(depth, 1, C_KV_RANK)
    fg = final_norm.reshape(1, d)
    head_of = np.arange(B_Q) // HEAD_DIM
    seg_q = jnp.asarray(head_of[:, None] == head_of[None, :], BF16)
    seg_k = seg_q[:B_KV, :B_KV]
    tables = _rope_tables(s)
    slopes = [2.0 ** (-8.0 * (i + 1) / A_HEADS) for i in range(A_HEADS)]

    xs = x.reshape(t, d)
    n_m = t // tm
    pos_blocks = s // tm
    row = lambda w: pl.BlockSpec((tm, w), lambda i: (i, 0))
    tab = pl.BlockSpec((tm, LANES), lambda i: (i % pos_blocks, 0))

    for l in range(depth):
        widths = (A_Q, A_KV, A_KV, B_Q, B_KV, B_KV,
                  C_HEADS * C_HEAD_PAD, C_HEADS * C_HEAD_PAD, C_HEADS * C_V)
        qa, ka, va, qb, kb, vb, qc, kc, vc = pl.pallas_call(
            _proj_in_kernel,
            out_shape=[jax.ShapeDtypeStruct((t, w), BF16) for w in widths],
            grid=(n_m,),
            in_specs=[row(d), _layer_const((1, d), l), _layer_const((d, IN_COLS_PAD), l),
                      _const((B_Q, B_Q)), _const((B_KV, B_KV)),
                      _layer_const((1, B_Q), l), _layer_const((1, B_KV), l),
                      _layer_const((1, C_Q_RANK), l), _layer_const((1, C_KV_RANK), l),
                      _layer_const((C_Q_RANK, C_HEADS * C_HEAD_PAD), l),
                      _layer_const((C_KV_RANK, C_HEADS * C_HEAD_PAD), l),
                      _layer_const((C_KV_RANK, C_HEADS * C_V), l),
                      tab, tab, tab, tab, tab, tab],
            out_specs=[row(w) for w in widths],
            compiler_params=_params(("parallel",)),
            name=f"proj_in_{l}",
        )(xs, attn_g, w_in_b, seg_q, seg_k, gq, gk, cqn, ckvn, uq, ukv_k, ukv_v, *tables)

        nb = s // BLOCK
        oa = pl.pallas_call(
            functools.partial(_attn_a_kernel, seq=s, slopes=slopes),
            out_shape=jax.ShapeDtypeStruct((t, A_Q), BF16),
            grid=(b, nb),
            in_specs=[pl.BlockSpec(memory_space=pltpu.SMEM),
                      pl.BlockSpec((BLOCK, A_Q), lambda i, j: (i * nb + j, 0)),
                      pl.BlockSpec((s, A_KV), lambda i, j: (i, 0)),
                      pl.BlockSpec((s, A_KV), lambda i, j: (i, 0))],
            out_specs=pl.BlockSpec((BLOCK, A_Q), lambda i, j: (i * nb + j, 0)),
            compiler_params=_params(("parallel", "arbitrary")),
            name=f"attn_a_{l}",
        )(a_sink[l], qa, ka, va)

        nqb = s // tq_b
        ob = pl.pallas_call(
            functools.partial(_attn_b_kernel, tq=tq_b),
            out_shape=jax.ShapeDtypeStruct((t, B_Q), BF16),
            grid=(b, nqb),
            in_specs=[pl.BlockSpec((tq_b, B_Q), lambda i, j: (i * nqb + j, 0)),
                      pl.BlockSpec((s, B_KV), lambda i, j: (i, 0)),
                      pl.BlockSpec((s, B_KV), lambda i, j: (i, 0))],
            out_specs=pl.BlockSpec((tq_b, B_Q), lambda i, j: (i * nqb + j, 0)),
            compiler_params=_params(("parallel", "arbitrary")),
            name=f"attn_b_{l}",
        )(qb, kb, vb)

        nqc = s // tq_c
        oc = pl.pallas_call(
            _attn_c_kernel,
            out_shape=jax.ShapeDtypeStruct((t, C_HEADS * C_V), BF16),
            grid=(b, nqc),
            in_specs=[pl.BlockSpec((tq_c, C_HEADS * C_HEAD_PAD), lambda i, j: (i * nqc + j, 0)),
                      pl.BlockSpec((s, C_HEADS * C_HEAD_PAD), lambda i, j: (i, 0)),
                      pl.BlockSpec((s, C_HEADS * C_V), lambda i, j: (i, 0))],
            out_specs=pl.BlockSpec((tq_c, C_HEADS * C_V), lambda i, j: (i * nqc + j, 0)),
            compiler_params=_params(("parallel", "arbitrary")),
            name=f"attn_c_{l}",
        )(qc, kc, vc)

        xs = pl.pallas_call(
            functools.partial(_out_ffn_kernel, tf=tf, final=(l == depth - 1)),
            out_shape=jax.ShapeDtypeStruct((t, d), F32),
            grid=(n_m,),
            in_specs=[row(d), row(A_Q), row(B_Q), row(C_HEADS * C_V),
                      _layer_const((d, d), l), _layer_const((1, d), l),
                      _layer_const((d, D_FF), l), _layer_const((D_FF, d), l),
                      _const((1, d))],
            out_specs=row(d),
            compiler_params=_params(("parallel",)),
            name=f"out_ffn_{l}",
        )(xs, oa, ob, oc, w_out_b, mlp_g, w1_b, w2_b, fg)

    return xs.reshape(b, s, d)
```

```python
import functools
import math

import numpy as np
import jax
import jax.numpy as jnp
from jax import lax
from jax.experimental import pallas as pl
from jax.experimental.pallas import tpu as pltpu

D_MODEL = 1024
HEAD_DIM = 64
BLOCK = 128
EPS = 1e-6
ROPE_BASE = 10000.0
A_HEADS, A_KV_HEADS, WINDOW = 6, 2, 128
B_HEADS, B_KV_HEADS, GRID_W = 6, 2, 64
C_HEADS, C_Q_RANK, C_KV_RANK, C_NOPE, C_ROPE, C_V = 4, 256, 128, 64, 32, 64
D_FF = 4 * D_MODEL

A_Q = A_HEADS * HEAD_DIM
A_KV = A_KV_HEADS * HEAD_DIM
B_Q = B_HEADS * HEAD_DIM
B_KV = B_KV_HEADS * HEAD_DIM
IN_COLS = 2 * (A_Q + 2 * A_KV) + C_Q_RANK + C_KV_RANK + C_ROPE

LANES = 128
IN_COLS_PAD = 1792
C_HEAD_PAD = 128
LOG2E = math.log2(math.e)
VMEM_LIMIT = 56 * 1024 * 1024

O1 = A_Q
O2 = O1 + A_KV
O3 = O2 + A_KV
O4 = O3 + B_Q
O5 = O4 + B_KV
O6 = O5 + B_KV
O7 = O6 + C_Q_RANK
O8 = O7 + C_KV_RANK

BF16 = jnp.bfloat16
F32 = jnp.float32


def _rmsnorm(x, g):
    return x * lax.rsqrt(jnp.mean(x * x, axis=-1, keepdims=True) + EPS) * g


def _rope(t, cos, sin_next, sin_prev):
    return (t * cos + pltpu.roll(t, LANES - 16, axis=1) * sin_next
            + pltpu.roll(t, 16, axis=1) * sin_prev)


def _group_sumsq(x, seg):
    sq = x * x
    hi = sq.astype(BF16)
    lo = (sq - hi.astype(F32)).astype(BF16)
    return (jnp.dot(hi, seg, preferred_element_type=F32)
            + jnp.dot(lo, seg, preferred_element_type=F32))


def _proj_in_kernel(x_ref, g_ref, w_ref, segq_ref, segk_ref, gq_ref, gk_ref,
                    cqn_ref, ckvn_ref, wuq_ref, wukvk_ref, wukvv_ref,
                    cosb_ref, snb_ref, spb_ref, cosc_ref, snc_ref, spc_ref,
                    qa_ref, ka_ref, va_ref, qb_ref, kb_ref, vb_ref,
                    qc_ref, kc_ref, vc_ref):
    x = x_ref[...]
    h = _rmsnorm(x, g_ref[...]).astype(BF16)
    p = jnp.dot(h, w_ref[...], preferred_element_type=F32)

    qa_ref[...] = (p[:, :O1] * (HEAD_DIM ** -0.5 * LOG2E)).astype(BF16)
    ka_ref[...] = p[:, O1:O2].astype(BF16)
    va_ref[...] = p[:, O2:O3].astype(BF16)

    cosb, snb, spb = cosb_ref[...], snb_ref[...], spb_ref[...]
    xq = p[:, O3:O4]
    qn = xq * lax.rsqrt(_group_sumsq(xq, segq_ref[...]) * (1.0 / HEAD_DIM) + EPS) * gq_ref[...]
    for c in range(B_Q // LANES):
        t = _rope(qn[:, c * LANES:(c + 1) * LANES], cosb, snb, spb)
        qb_ref[:, c * LANES:(c + 1) * LANES] = (t * (HEAD_DIM ** -0.5 * LOG2E)).astype(BF16)
    xk = p[:, O4:O5]
    kn = xk * lax.rsqrt(_group_sumsq(xk, segk_ref[...]) * (1.0 / HEAD_DIM) + EPS) * gk_ref[...]
    kb_ref[...] = _rope(kn, cosb, snb, spb).astype(BF16)
    vb_ref[...] = p[:, O5:O6].astype(BF16)

    cosc, snc, spc = cosc_ref[...], snc_ref[...], spc_ref[...]
    cq = jnp.dot(_rmsnorm(p[:, O6:O7], cqn_ref[...]).astype(BF16), wuq_ref[...],
                 preferred_element_type=F32)
    kvl = _rmsnorm(p[:, O7:O8], ckvn_ref[...]).astype(BF16)
    kn_c = jnp.dot(kvl, wukvk_ref[...], preferred_element_type=F32)
    vc_ref[...] = jnp.dot(kvl, wukvv_ref[...], preferred_element_type=F32).astype(BF16)
    kr = _rope(pltpu.roll(p[:, O8:O8 + LANES], C_NOPE, axis=1), cosc, snc, spc)
    c_scale = (C_NOPE + C_ROPE) ** -0.5 * LOG2E
    for hd in range(C_HEADS):
        sl = slice(hd * C_HEAD_PAD, (hd + 1) * C_HEAD_PAD)
        qc_ref[:, sl] = (_rope(cq[:, sl], cosc, snc, spc) * c_scale).astype(BF16)
        kc_ref[:, sl] = (kn_c[:, sl] + kr).astype(BF16)


def _init_pipeline(s1, e1, l1):
    s1[...] = jnp.zeros_like(s1)
    e1[...] = jnp.zeros_like(e1)
    l1[...] = jnp.ones_like(l1)


def _run_pipeline(j, step, s0, s1, e0, e1, l0, l1):
    @pl.when(j % 2 == 0)
    def _():
        step(s0, s1, e0, e1, l0, l1)

    @pl.when(j % 2 == 1)
    def _():
        step(s1, s0, e1, e0, l1, l0)


def _softmax_stage(s_r, e_w, l_w, n_chunks, chunk):
    for c in range(n_chunks):
        rows = slice(c * chunk, (c + 1) * chunk)
        s = s_r[rows, :]
        e = jnp.exp2(s - jnp.max(s, axis=-1, keepdims=True))
        l_w[rows, :] = jnp.sum(e, axis=-1, keepdims=True)
        e_w[rows, :] = e.astype(BF16)


def _attn_a_kernel(sink_ref, q_ref, k_ref, v_ref, o_ref, bias_ref, s0, s1, e0, e1, l0, l1,
                   *, seq, slopes, tq):
    g = A_HEADS // A_KV_HEADS
    span = 3 * BLOCK
    nblk = tq // BLOCK
    nb = seq // BLOCK
    tiles_per_seq = seq // tq
    j = pl.program_id(0)
    n_tiles = pl.num_programs(0) - 2

    @pl.when(j == 0)
    def _():
        _init_pipeline(s1, e1, l1)
        rel = (lax.broadcasted_iota(jnp.int32, (BLOCK, span), 0)
               - lax.broadcasted_iota(jnp.int32, (BLOCK, span), 1))
        for var in range(3):
            dist = jnp.abs(rel + var * BLOCK)
            for hd in range(A_HEADS):
                bias_ref[var * A_HEADS + hd] = jnp.where(
                    dist <= WINDOW, (-slopes[hd] * LOG2E) * dist.astype(F32), -1e30)

    def window(stage, bi):
        tile = jnp.clip(j - stage, 0, n_tiles - 1)
        n = (tile % tiles_per_seq) * nblk + bi
        return n, jnp.clip(n - 1, 0, nb - 3)

    def step(s_w, s_r, e_w, e_r, l_w, l_r):
        for bi in range(nblk):
            _, first = window(0, bi)
            start = pl.multiple_of(first * BLOCK, BLOCK)
            for kv in range(A_KV_HEADS):
                k = k_ref[pl.ds(start, span), kv * HEAD_DIM:(kv + 1) * HEAD_DIM]
                q = jnp.concatenate(
                    [q_ref[bi * BLOCK:(bi + 1) * BLOCK,
                           (kv * g + gi) * HEAD_DIM:(kv * g + gi + 1) * HEAD_DIM]
                     for gi in range(g)], axis=0)
                r0 = (bi * A_KV_HEADS + kv) * g * BLOCK
                s_w[r0:r0 + g * BLOCK, :] = lax.dot_general(
                    q, k, (((1,), (1,)), ((), ())), preferred_element_type=F32)
        for bi in range(nblk):
            n, first = window(1, bi)
            var = n - first
            for hd in range(A_HEADS):
                rows = slice((bi * A_HEADS + hd) * BLOCK, (bi * A_HEADS + hd + 1) * BLOCK)
                bias = bias_ref[var * A_HEADS + hd]
                s = jnp.where(bias > -1e29, s_r[rows, :] + bias, -1e30)
                sink = sink_ref[hd] * LOG2E
                m = jnp.maximum(jnp.max(s, axis=-1, keepdims=True), sink)
                e = jnp.exp2(s - m)
                l_w[rows, :] = jnp.sum(e, axis=-1, keepdims=True) + jnp.exp2(sink - m)
                e_w[rows, :] = e.astype(BF16)
        for bi in range(nblk):
            _, first = window(2, bi)
            start = pl.multiple_of(first * BLOCK, BLOCK)
            for kv in range(A_KV_HEADS):
                v = v_ref[pl.ds(start, span), kv * HEAD_DIM:(kv + 1) * HEAD_DIM]
                r0 = (bi * A_KV_HEADS + kv) * g * BLOCK
                o = (jnp.dot(e_r[r0:r0 + g * BLOCK, :], v, preferred_element_type=F32)
                     / l_r[r0:r0 + g * BLOCK, :])
                for gi in range(g):
                    hd = kv * g + gi
                    o_ref[bi * BLOCK:(bi + 1) * BLOCK, hd * HEAD_DIM:(hd + 1) * HEAD_DIM] = (
                        o[gi * BLOCK:(gi + 1) * BLOCK].astype(BF16))

    _run_pipeline(j, step, s0, s1, e0, e1, l0, l1)


def _attn_b_kernel(q_ref, k_ref, v_ref, o_ref, s0, s1, e0, e1, l0, l1, *, tq):
    g = B_HEADS // B_KV_HEADS
    j = pl.program_id(0)
    pl.when(j == 0)(functools.partial(_init_pipeline, s1, e1, l1))

    def step(s_w, s_r, e_w, e_r, l_w, l_r):
        for kv in range(B_KV_HEADS):
            k = k_ref[:, kv * HEAD_DIM:(kv + 1) * HEAD_DIM]
            q = jnp.concatenate(
                [q_ref[:, (kv * g + gi) * HEAD_DIM:(kv * g + gi + 1) * HEAD_DIM]
                 for gi in range(g)], axis=0)
            s_w[kv * g * tq:(kv + 1) * g * tq, :] = lax.dot_general(
                q, k, (((1,), (1,)), ((), ())), preferred_element_type=F32)
        _softmax_stage(s_r, e_w, l_w, B_HEADS, tq)
        for kv in range(B_KV_HEADS):
            v = v_ref[:, kv * HEAD_DIM:(kv + 1) * HEAD_DIM]
            rows = slice(kv * g * tq, (kv + 1) * g * tq)
            o = jnp.dot(e_r[rows, :], v, preferred_element_type=F32) / l_r[rows, :]
            for gi in range(g):
                hd = kv * g + gi
                o_ref[:, hd * HEAD_DIM:(hd + 1) * HEAD_DIM] = o[gi * tq:(gi + 1) * tq].astype(BF16)

    _run_pipeline(j, step, s0, s1, e0, e1, l0, l1)


def _attn_c_kernel(q_ref, k_ref, v_ref, o_ref, s0, s1, e0, e1, l0, l1, *, tq):
    j = pl.program_id(0)
    pl.when(j == 0)(functools.partial(_init_pipeline, s1, e1, l1))

    def step(s_w, s_r, e_w, e_r, l_w, l_r):
        for hd in range(C_HEADS):
            q = q_ref[:, hd * C_HEAD_PAD:(hd + 1) * C_HEAD_PAD]
            k = k_ref[:, hd * C_HEAD_PAD:(hd + 1) * C_HEAD_PAD]
            s_w[hd * tq:(hd + 1) * tq, :] = lax.dot_general(
                q, k, (((1,), (1,)), ((), ())), preferred_element_type=F32)
        _softmax_stage(s_r, e_w, l_w, C_HEADS, tq)
        for hd in range(C_HEADS):
            rows = slice(hd * tq, (hd + 1) * tq)
            v = v_ref[:, hd * C_V:(hd + 1) * C_V]
            o = jnp.dot(e_r[rows, :], v, preferred_element_type=F32) / l_r[rows, :]
            o_ref[:, hd * C_V:(hd + 1) * C_V] = o.astype(BF16)

    _run_pipeline(j, step, s0, s1, e0, e1, l0, l1)


def _out_ffn_kernel(x_ref, oa_ref, ob_ref, oc_ref, wo_ref, g_ref, w1_ref, w2_ref, fg_ref,
                    y_ref, *, tf, final):
    mixed = jnp.concatenate([oa_ref[...], ob_ref[...], oc_ref[...]], axis=-1)
    x = x_ref[...] + jnp.dot(mixed, wo_ref[...], preferred_element_type=F32)
    h = _rmsnorm(x, g_ref[...]).astype(BF16)
    mlp = None
    for f in range(D_FF // tf):
        u = jnp.maximum(jnp.dot(h, w1_ref[:, f * tf:(f + 1) * tf], preferred_element_type=F32), 0.0)
        part = jnp.dot((u * u).astype(BF16), w2_ref[f * tf:(f + 1) * tf, :],
                       preferred_element_type=F32)
        mlp = part if mlp is None else mlp + part
    acc = x + mlp
    if final:
        acc = _rmsnorm(acc, fg_ref[...])
    y_ref[...] = acc


def _rope_np(pos, dim):
    inv = ROPE_BASE ** (-np.arange(0, dim, 2, dtype=np.float64) / dim)
    ang = pos.astype(np.float64)[:, None] * inv[None, :]
    ang = np.concatenate([ang, ang], axis=-1)
    return np.cos(ang), np.sin(ang)


def _split_sin(sin, width):
    first = (np.arange(width) % 32) < 16
    return np.where(first, -sin, 0.0), np.where(first, 0.0, sin)


def _rope_tables(seq):
    pos = np.arange(seq)
    half = HEAD_DIM // 2
    cr, sr = _rope_np(pos // GRID_W, half)
    cc, sc = _rope_np(pos % GRID_W, half)
    cos_b = np.tile(np.concatenate([cr, cc], axis=-1), (1, LANES // HEAD_DIM))
    sin_b = np.tile(np.concatenate([sr, sc], axis=-1), (1, LANES // HEAD_DIM))
    cm, sm = _rope_np(pos, C_ROPE)
    cos_c = np.ones((seq, C_HEAD_PAD))
    sin_c = np.zeros((seq, C_HEAD_PAD))
    cos_c[:, C_NOPE:C_NOPE + C_ROPE] = cm
    sin_c[:, C_NOPE:C_NOPE + C_ROPE] = sm
    out = []
    for cos, sin in ((cos_b, sin_b), (cos_c, sin_c)):
        sn, sp = _split_sin(sin, LANES)
        out += [jnp.asarray(cos, F32), jnp.asarray(sn, F32), jnp.asarray(sp, F32)]
    return out


def _const(shape):
    return pl.BlockSpec(shape, lambda *_: (0,) * len(shape), pipeline_mode=pl.Buffered(1))


def _layer_const(shape, l):
    return pl.BlockSpec((None,) + shape, lambda *_: (l,) + (0,) * len(shape),
                        pipeline_mode=pl.Buffered(1))


def _params(sem):
    return pltpu.CompilerParams(dimension_semantics=sem, vmem_limit_bytes=VMEM_LIMIT)


def _tile_of(n_tiles, stage):
    return lambda j: jnp.clip(j - stage, 0, n_tiles - 1)


def _pipeline_scratch(rows, cols):
    return ([pltpu.VMEM((rows, cols), F32)] * 2 + [pltpu.VMEM((rows, cols), BF16)] * 2
            + [pltpu.VMEM((rows, 1), F32)] * 2)


def kernel(x, attn_norm, w_in, a_sink, b_q_norm, b_k_norm, c_q_norm, c_kv_norm,
           w_uq, w_ukv, w_out, mlp_norm, w_ff1, w_ff2, final_norm):
    b, s, d = x.shape
    depth = w_in.shape[0]
    t = b * s
    tm = 512
    tq_a = 512
    tq_b = 128
    tq_c = 256
    tf = 1024
    assert s % tm == 0 and s % tq_a == 0 and s % tq_b == 0 and s % tq_c == 0
    assert tq_a % BLOCK == 0 and s >= 3 * BLOCK

    w_in_b = jnp.pad(w_in.astype(BF16), ((0, 0), (0, 0), (0, IN_COLS_PAD - IN_COLS)))
    w_out_b = w_out.astype(BF16)
    w1_b = w_ff1.astype(BF16)
    w2_b = w_ff2.astype(BF16)
    uq = w_uq.astype(BF16).reshape(depth, C_Q_RANK, C_HEADS, C_NOPE + C_ROPE)
    uq = jnp.pad(uq, ((0, 0), (0, 0), (0, 0), (0, C_HEAD_PAD - C_NOPE - C_ROPE)))
    uq = uq.reshape(depth, C_Q_RANK, C_HEADS * C_HEAD_PAD)
    ukv = w_ukv.astype(BF16).reshape(depth, C_KV_RANK, C_HEADS, C_NOPE + C_V)
    ukv_k = jnp.pad(ukv[..., :C_NOPE], ((0, 0), (0, 0), (0, 0), (0, C_HEAD_PAD - C_NOPE)))
    ukv_k = ukv_k.reshape(depth, C_KV_RANK, C_HEADS * C_HEAD_PAD)
    ukv_v = ukv[..., C_NOPE:].reshape(depth, C_KV_RANK, C_HEADS * C_V)
    gq = jnp.tile(b_q_norm, (1, B_HEADS)).reshape(depth, 1, B_Q)
    gk = jnp.tile(b_k_norm, (1, B_KV_HEADS)).reshape(depth, 1, B_KV)
    attn_g = attn_norm.reshape(depth, 1, d)
    mlp_g = mlp_norm.reshape(depth, 1, d)
    cqn = c_q_norm.reshape(depth, 1, C_Q_RANK)
    ckvn = c_kv_norm.reshape(depth, 1, C_KV_RANK)
    fg = final_norm.reshape(1, d)
    head_of = np.arange(B_Q) // HEAD_DIM
    seg_q = jnp.asarray(head_of[:, None] == head_of[None, :], BF16)
    seg_k = seg_q[:B_KV, :B_KV]
    tables = _rope_tables(s)
    slopes = [2.0 ** (-8.0 * (i + 1) / A_HEADS) for i in range(A_HEADS)]

    xs = x.reshape(t, d)
    n_m = t // tm
    pos_blocks = s // tm
    row = lambda w: pl.BlockSpec((tm, w), lambda i: (i, 0))
    tab = pl.BlockSpec((tm, LANES), lambda i: (i % pos_blocks, 0))

    for l in range(depth):
        widths = (A_Q, A_KV, A_KV, B_Q, B_KV, B_KV,
                  C_HEADS * C_HEAD_PAD, C_HEADS * C_HEAD_PAD, C_HEADS * C_V)
        qa, ka, va, qb, kb, vb, qc, kc, vc = pl.pallas_call(
            _proj_in_kernel,
            out_shape=[jax.ShapeDtypeStruct((t, w), BF16) for w in widths],
            grid=(n_m,),
            in_specs=[row(d), _layer_const((1, d), l), _layer_const((d, IN_COLS_PAD), l),
                      _const((B_Q, B_Q)), _const((B_KV, B_KV)),
                      _layer_const((1, B_Q), l), _layer_const((1, B_KV), l),
                      _layer_const((1, C_Q_RANK), l), _layer_const((1, C_KV_RANK), l),
                      _layer_const((C_Q_RANK, C_HEADS * C_HEAD_PAD), l),
                      _layer_const((C_KV_RANK, C_HEADS * C_HEAD_PAD), l),
                      _layer_const((C_KV_RANK, C_HEADS * C_V), l),
                      tab, tab, tab, tab, tab, tab],
            out_specs=[row(w) for w in widths],
            compiler_params=_params(("parallel",)),
            name=f"proj_in_{l}",
        )(xs, attn_g, w_in_b, seg_q, seg_k, gq, gk, cqn, ckvn, uq, ukv_k, ukv_v, *tables)

        n_a = t // tq_a
        per_a = s // tq_a
        q_a, v_a = _tile_of(n_a, 0), _tile_of(n_a, 2)
        oa = pl.pallas_call(
            functools.partial(_attn_a_kernel, seq=s, slopes=slopes, tq=tq_a),
            out_shape=jax.ShapeDtypeStruct((t, A_Q), BF16),
            grid=(n_a + 2,),
            in_specs=[pl.BlockSpec(memory_space=pltpu.SMEM),
                      pl.BlockSpec((tq_a, A_Q), lambda j: (q_a(j), 0)),
                      pl.BlockSpec((s, A_KV), lambda j: (q_a(j) // per_a, 0)),
                      pl.BlockSpec((s, A_KV), lambda j: (v_a(j) // per_a, 0))],
            out_specs=pl.BlockSpec((tq_a, A_Q), lambda j: (v_a(j), 0)),
            scratch_shapes=[pltpu.VMEM((3 * A_HEADS, BLOCK, 3 * BLOCK), F32)]
                           + _pipeline_scratch(A_HEADS * tq_a, 3 * BLOCK),
            compiler_params=_params(("arbitrary",)),
            name=f"attn_a_{l}",
        )(a_sink[l], qa, ka, va)

        n_b = t // tq_b
        per_b = s // tq_b
        q_b, v_b = _tile_of(n_b, 0), _tile_of(n_b, 2)
        ob = pl.pallas_call(
            functools.partial(_attn_b_kernel, tq=tq_b),
            out_shape=jax.ShapeDtypeStruct((t, B_Q), BF16),
            grid=(n_b + 2,),
            in_specs=[pl.BlockSpec((tq_b, B_Q), lambda j: (q_b(j), 0)),
                      pl.BlockSpec((s, B_KV), lambda j: (q_b(j) // per_b, 0)),
                      pl.BlockSpec((s, B_KV), lambda j: (v_b(j) // per_b, 0))],
            out_specs=pl.BlockSpec((tq_b, B_Q), lambda j: (v_b(j), 0)),
            scratch_shapes=_pipeline_scratch(B_HEADS * tq_b, s),
            compiler_params=_params(("arbitrary",)),
            name=f"attn_b_{l}",
        )(qb, kb, vb)

        n_c = t // tq_c
        per_c = s // tq_c
        q_c, v_c = _tile_of(n_c, 0), _tile_of(n_c, 2)
        oc = pl.pallas_call(
            functools.partial(_attn_c_kernel, tq=tq_c),
            out_shape=jax.ShapeDtypeStruct((t, C_HEADS * C_V), BF16),
            grid=(n_c + 2,),
            in_specs=[pl.BlockSpec((tq_c, C_HEADS * C_HEAD_PAD), lambda j: (q_c(j), 0)),
                      pl.BlockSpec((s, C_HEADS * C_HEAD_PAD), lambda j: (q_c(j) // per_c, 0)),
                      pl.BlockSpec((s, C_HEADS * C_V), lambda j: (v_c(j) // per_c, 0))],
            out_specs=pl.BlockSpec((tq_c, C_HEADS * C_V), lambda j: (v_c(j), 0)),
            scratch_shapes=_pipeline_scratch(C_HEADS * tq_c, s),
            compiler_params=_params(("arbitrary",)),
            name=f"attn_c_{l}",
        )(qc, kc, vc)

        xs = pl.pallas_call(
            functools.partial(_out_ffn_kernel, tf=tf, final=(l == depth - 1)),
            out_shape=jax.ShapeDtypeStruct((t, d), F32),
            grid=(n_m,),
            in_specs=[row(d), row(A_Q), row(B_Q), row(C_HEADS * C_V),
                      _layer_const((d, d), l), _layer_const((1, d), l),
                      _layer_const((d, D_FF), l), _layer_const((D_FF, d), l),
                      _const((1, d))],
            out_specs=row(d),
            compiler_params=_params(("parallel",)),
            name=f"out_ffn_{l}",
        )(xs, oa, ob, oc, w_out_b, mlp_g, w1_b, w2_b, fg)

    return xs.reshape(b, s, d)
```
